```python
import math
import jax, jax.numpy as jnp
from jax import lax
import numpy as np

D_MODEL = 4096
BATCH = 2
SEQ = 8192
DEPTH = 2

A_HEAD_DIM = 128
A_WIDTH = D_MODEL // 2
A_HEADS = A_WIDTH // A_HEAD_DIM
IDX_HEADS = 16
IDX_DIM = 64
TOPK_MAX = 256
Q_BLOCK = 128
REL_BUCKETS = 32
REL_MAX_DIST = 128
POOL_WINDOWS = (2, 4, 8, 16)
POOL_GROUPS = 4
POOL_WIDTH = D_MODEL // 4
POOL_GROUP_DIM = POOL_WIDTH // POOL_GROUPS
RET_WIDTH = D_MODEL // 4
RET_HEADS = 4
RET_HEAD_DIM = RET_WIDTH // RET_HEADS
RET_CHUNK = 128
ROPE_BASE = 10000.0
D_FF = 11008
CONV_WIDTH = 3
NORM_EPS = 1e-6

N_BRANCHES = 3
SPLIT_SIZES = (A_WIDTH, A_WIDTH, A_WIDTH, IDX_HEADS * IDX_DIM, IDX_DIM, IDX_HEADS,
               POOL_WIDTH, RET_WIDTH, RET_WIDTH, RET_WIDTH, RET_WIDTH, N_BRANCHES * D_MODEL)
IN_WIDTH = sum(SPLIT_SIZES)

kernel_name = "hybrid_dsa_pool_retention_gated_block"


def rms_norm(x, g):
    x32 = x.astype(jnp.float32)
    y = x32 * lax.rsqrt(jnp.mean(x32 * x32, axis=-1, keepdims=True) + NORM_EPS)
    return (y * g.astype(jnp.float32)).astype(x.dtype)


def head_rms(x):
    x32 = x.astype(jnp.float32)
    return x32 * lax.rsqrt(jnp.mean(x32 * x32, axis=-1, keepdims=True) + NORM_EPS)


def t5_bucket(rel):
    max_exact = REL_BUCKETS // 2
    n = jnp.maximum(rel, 0)
    nf = jnp.maximum(n, 1).astype(jnp.float32)
    large = max_exact + (jnp.log(nf / max_exact) / math.log(REL_MAX_DIST / max_exact)
                         * (REL_BUCKETS - max_exact)).astype(jnp.int32)
    large = jnp.minimum(large, REL_BUCKETS - 1)
    return jnp.where(n < max_exact, n, large)


def dsa_attention(q, k, v, iq, ik, iw, rel_bias):
    B, S, H, Dh = q.shape
    k_sel = min(TOPK_MAX, S // 4)
    nb = S // Q_BLOCK
    pos = jnp.arange(S, dtype=jnp.int32)
    scale = Dh ** -0.5

    def blockify(a):
        return jnp.swapaxes(a.reshape((B, nb, Q_BLOCK) + a.shape[2:]), 0, 1)

    def one_block(args):
        qb, iqb, iwb, tb = args
        logits = jnp.einsum('bqhd,bsd->bqhs', iqb, ik)
        score = jnp.einsum('bqh,bqhs->bqs', iwb, jax.nn.relu(logits)).astype(jnp.float32)
        score = jnp.where(pos[None, None, :] <= tb[None, :, None], score, -jnp.inf)
        _, idx = lax.top_k(score, k_sel)
        kg = jax.vmap(lambda kk, ii: kk[ii])(k, idx)
        vg = jax.vmap(lambda vv, ii: vv[ii])(v, idx)
        rel = tb[None, :, None] - idx
        bias = rel_bias[t5_bucket(rel)].astype(jnp.float32)
        s = jnp.einsum('bqhd,bqkhd->bqhk', qb, kg).astype(jnp.float32) * scale
        s = s + jnp.transpose(bias, (0, 1, 3, 2))
        valid = (rel >= 0)[:, :, None, :]
        s = jnp.where(valid, s, -jnp.inf)
        p = jax.nn.softmax(s, axis=-1).astype(v.dtype)
        return jnp.einsum('bqhk,bqkhd->bqhd', p, vg)

    out = lax.map(one_block, (blockify(q), blockify(iq), blockify(iw), pos.reshape(nb, Q_BLOCK)))
    return jnp.swapaxes(out, 0, 1).reshape(B, S, H * Dh)


def multiscale_pool(u, pool_w, pool_scale):
    B, S, C = u.shape
    u32 = u.astype(jnp.float32).reshape(B, S, POOL_GROUPS, POOL_GROUP_DIM)
    csum = jnp.concatenate([jnp.zeros((B, 1, POOL_GROUPS, POOL_GROUP_DIM), jnp.float32),
                            jnp.cumsum(u32, axis=1)], axis=1)
    t = jnp.arange(S, dtype=jnp.int32)[:, None]
    win = jnp.array(POOL_WINDOWS, dtype=jnp.int32)[None, :]
    start = jnp.maximum(t + 1 - win, 0)
    count = (t + 1 - start).astype(jnp.float32)
    lower = csum[:, start, jnp.arange(POOL_GROUPS)[None, :]]
    pooled = (csum[:, 1:] - lower) / count[None, :, :, None]
    mixed = jnp.einsum('bsgc,gcd->bsgd', pooled - u32, pool_w.astype(jnp.float32))
    return (mixed.reshape(B, S, C) * pool_scale.astype(jnp.float32)).astype(u.dtype)


def rotary(x):
    B, S, H, D = x.shape
    half = D // 2
    inv_freq = 1.0 / (ROPE_BASE ** jnp.linspace(0.0, 1.0, half, dtype=jnp.float32))
    ang = jnp.arange(S, dtype=jnp.float32)[:, None] * inv_freq[None, :]
    cos = jnp.cos(ang)[None, :, None, :]
    sin = jnp.sin(ang)[None, :, None, :]
    x32 = x.astype(jnp.float32)
    x1, x2 = x32[..., :half], x32[..., half:]
    return jnp.concatenate([x1 * cos - x2 * sin, x1 * sin + x2 * cos], axis=-1)


def retention(q, k, v):
    B, S, H, Dk = q.shape
    Dv = v.shape[-1]
    C = RET_CHUNK
    nc = S // C
    log_g = jnp.log(1.0 - jnp.exp2(-5.0 - jnp.arange(H, dtype=jnp.float32)))
    i = jnp.arange(C, dtype=jnp.float32)
    diff = i[:, None] - i[None, :]
    decay_mask = jnp.where(diff[None] >= 0,
                           jnp.exp(jnp.maximum(diff, 0.0)[None] * log_g[:, None, None]), 0.0)
    q_decay = jnp.exp((i + 1.0)[:, None] * log_g[None, :])
    k_decay = jnp.exp((C - 1.0 - i)[:, None] * log_g[None, :])
    chunk_decay = jnp.exp(C * log_g)

    def chunks(a):
        return jnp.swapaxes(a.reshape(B, nc, C, H, a.shape[-1]), 0, 1)

    def step(state, inp):
        qc, kc, vc = inp
        s = jnp.einsum('bihk,bjhk->bhij', qc, kc) * decay_mask
        inner = jnp.einsum('bhij,bjhv->bihv', s, vc)
        cross = jnp.einsum('bihk,bhkv->bihv', qc * q_decay[:, :, None], state)
        state = state * chunk_decay[:, None, None] + jnp.einsum(
            'bjhk,bjhv->bhkv', kc * k_decay[:, :, None], vc)
        return state, inner + cross

    init = jnp.zeros((B, H, Dk, Dv), jnp.float32)
    _, out = lax.scan(step, init, (chunks(q), chunks(k), chunks(v)))
    return jnp.swapaxes(out, 0, 1).reshape(B, S, H, Dv)


def conv_glu_ffn(h, w_up, conv_w, w_down):
    a, b = jnp.split(h @ w_up, 2, axis=-1)
    a = lax.conv_general_dilated(a, conv_w[:, None, :], window_strides=(1,),
                                 padding=[(CONV_WIDTH - 1, 0)],
                                 dimension_numbers=('NWC', 'WIO', 'NWC'),
                                 feature_group_count=D_FF)
    return (jax.nn.silu(a) * b) @ w_down


def setup_inputs(seed: int = 0) -> dict:
    key = jax.random.key(seed)
    ks = jax.random.split(key, 16)
    f32 = jnp.float32

    def nrm(k, shape, scale):
        return jax.random.normal(k, shape, f32) * scale

    return {
        "x": nrm(ks[0], (BATCH, SEQ, D_MODEL), 1.0),
        "norm1_g": 1.0 + nrm(ks[1], (DEPTH, D_MODEL), 0.1),
        "w_in": nrm(ks[2], (DEPTH, D_MODEL, IN_WIDTH), D_MODEL ** -0.5),
        "w_br_a": nrm(ks[3], (DEPTH, A_WIDTH, D_MODEL), A_WIDTH ** -0.5),
        "pool_w": nrm(ks[4], (DEPTH, POOL_GROUPS, POOL_GROUP_DIM, POOL_GROUP_DIM), POOL_GROUP_DIM ** -0.5),
        "pool_scale": 1.0 + nrm(ks[5], (DEPTH, POOL_WIDTH), 0.1),
        "w_br_b": nrm(ks[6], (DEPTH, POOL_WIDTH, D_MODEL), POOL_WIDTH ** -0.5),
        "w_br_c": nrm(ks[7], (DEPTH, RET_WIDTH, D_MODEL), RET_WIDTH ** -0.5),
        "w_out": nrm(ks[8], (DEPTH, D_MODEL, D_MODEL), D_MODEL ** -0.5),
        "rel_bias": nrm(ks[9], (REL_BUCKETS, A_HEADS), 0.5),
        "norm2_g": 1.0 + nrm(ks[10], (DEPTH, D_MODEL), 0.1),
        "w_ffn_up": nrm(ks[11], (DEPTH, D_MODEL, 2 * D_FF), D_MODEL ** -0.5),
        "ffn_conv": nrm(ks[12], (DEPTH, CONV_WIDTH, D_FF), CONV_WIDTH ** -0.5),
        "w_ffn_down": nrm(ks[13], (DEPTH, D_FF, D_MODEL), D_FF ** -0.5),
        "final_g": 1.0 + nrm(ks[14], (D_MODEL,), 0.1),
    }


def reference(x, norm1_g, w_in, w_br_a, pool_w, pool_scale, w_br_b, w_br_c, w_out,
              rel_bias, norm2_g, w_ffn_up, ffn_conv, w_ffn_down, final_g):
    B, S, D = x.shape
    offsets = np.cumsum(SPLIT_SIZES)[:-1].tolist()
    for l in range(DEPTH):
        h = rms_norm(x, norm1_g[l])
        z = h @ w_in[l]
        (aq, ak, av, iq, ik, iw, pu, rq, rk, rv, rg, gates) = jnp.split(z, offsets, axis=-1)

        o_a = dsa_attention(aq.reshape(B, S, A_HEADS, A_HEAD_DIM),
                            ak.reshape(B, S, A_HEADS, A_HEAD_DIM),
                            av.reshape(B, S, A_HEADS, A_HEAD_DIM),
                            iq.reshape(B, S, IDX_HEADS, IDX_DIM) * (IDX_DIM ** -0.5),
                            ik,
                            iw * (IDX_HEADS ** -0.5),
                            rel_bias)
        y_a = o_a @ w_br_a[l]

        y_b = multiscale_pool(pu, pool_w[l], pool_scale[l]) @ w_br_b[l]

        qr = rotary(rq.reshape(B, S, RET_HEADS, RET_HEAD_DIM))
        kr = rotary(rk.reshape(B, S, RET_HEADS, RET_HEAD_DIM)) * (RET_HEAD_DIM ** -0.5)
        vr = rv.reshape(B, S, RET_HEADS, RET_HEAD_DIM).astype(jnp.float32)
        ret = head_rms(retention(qr, kr, vr)).reshape(B, S, RET_WIDTH)
        o_c = (ret * jax.nn.silu(rg.astype(jnp.float32))).astype(x.dtype)
        y_c = o_c @ w_br_c[l]

        g = jax.nn.sigmoid(gates.reshape(B, S, N_BRANCHES, D))
        m = g[:, :, 0] * y_a + g[:, :, 1] * y_b + g[:, :, 2] * y_c
        x = x + m @ w_out[l]

        x = x + conv_glu_ffn(rms_norm(x, norm2_g[l]), w_ffn_up[l], ffn_conv[l], w_ffn_down[l])
    return rms_norm(x, final_g)
```

```python
import functools
import math

import jax
import jax.numpy as jnp
import numpy as np
from jax import lax
from jax.experimental import pallas as pl
from jax.experimental.pallas import tpu as pltpu

F32 = jnp.float32
BF16 = jnp.bfloat16

NORM_EPS = 1e-6
A_HEAD_DIM = 128
IDX_HEADS = 16
IDX_DIM = 64
TOPK_MAX = 256
REL_BUCKETS = 32
REL_MAX_DIST = 128
POOL_WINDOWS = (2, 4, 8, 16)
RET_HEADS = 4
RET_CHUNK = 128
ROPE_BASE = 10000.0
N_BRANCHES = 3

V7X_LANES = 128
V7X_SUBLANES = 8
V7X_VMEM_BYTES = 64 * 1024 * 1024
V7X_VMEM_REQUEST_CAP = V7X_VMEM_BYTES - 8 * 1024 * 1024

MASK_NEG = -1e30
KEY_OF_NEG_INF = -2139095041
INT32_MIN = -2147483648


def _pick(n, cands):
    for c in cands:
        if c <= n and n % c == 0:
            return c
    raise ValueError(f"no tile in {cands} divides {n}")


def _params(sem, vmem_bytes):
    limit = int(min(max(vmem_bytes * 5 // 4, 32 * 1024 * 1024), V7X_VMEM_REQUEST_CAP))
    return pltpu.CompilerParams(dimension_semantics=sem, vmem_limit_bytes=limit)


def _rmsnorm_kernel(x_ref, g_ref, o_ref):
    x = x_ref[...]
    ms = jnp.mean(x * x, axis=-1, keepdims=True)
    o_ref[...] = (x * lax.rsqrt(ms + NORM_EPS) * g_ref[...]).astype(o_ref.dtype)


def _rmsnorm(x, g, out_dtype):
    T, D = x.shape
    tr = _pick(T, (256, 128, 64, 32, 16, 8))
    est = 2 * tr * D * (4 + jnp.dtype(out_dtype).itemsize) + 2 * tr * D * 4
    return pl.pallas_call(
        _rmsnorm_kernel,
        out_shape=jax.ShapeDtypeStruct((T, D), out_dtype),
        grid=(T // tr,),
        in_specs=[pl.BlockSpec((tr, D), lambda i: (i, 0)), pl.BlockSpec((1, D), lambda i: (0, 0))],
        out_specs=pl.BlockSpec((tr, D), lambda i: (i, 0)),
        compiler_params=_params(("parallel",), est),
        name="rmsnorm",
    )(x, g.reshape(1, D).astype(F32))


def _mm_kernel(a_ref, b_ref, o_ref):
    o_ref[...] = jnp.dot(a_ref[...], b_ref[...], preferred_element_type=F32).astype(o_ref.dtype)


def _mm_res_kernel(a_ref, b_ref, r_ref, o_ref, acc_ref):
    k = pl.program_id(2)

    @pl.when(k == 0)
    def _():
        acc_ref[...] = r_ref[...]

    acc_ref[...] += jnp.dot(a_ref[...], b_ref[...], preferred_element_type=F32)

    @pl.when(k == pl.num_programs(2) - 1)
    def _():
        o_ref[...] = acc_ref[...]


def _matmul(a, b, out_dtype):
    M, K = a.shape
    N = b.shape[1]
    bm = _pick(M, (1024, 512, 256, 128))
    bn = _pick(N, (1024, 512, 256, 128))
    osz = jnp.dtype(out_dtype).itemsize
    est = 2 * (bm * K * 2 + K * bn * 2 + bm * bn * osz) + bm * bn * 4
    return pl.pallas_call(
        _mm_kernel,
        out_shape=jax.ShapeDtypeStruct((M, N), out_dtype),
        grid=(M // bm, N // bn),
        in_specs=[pl.BlockSpec((bm, K), lambda i, j: (i, 0)), pl.BlockSpec((K, bn), lambda i, j: (0, j))],
        out_specs=pl.BlockSpec((bm, bn), lambda i, j: (i, j)),
        compiler_params=_params(("parallel", "parallel"), est),
        name="matmul",
    )(a, b)


def _matmul_res(a, b, res):
    M, K = a.shape
    N = b.shape[1]
    bm = _pick(M, (1024, 512, 256, 128))
    bn = _pick(N, (1024, 512, 256, 128))
    nk = 1
    for cand in (1, 2, 4, 8):
        if K % (cand * V7X_LANES) == 0 and K // cand <= 3072:
            nk = cand
            break
    bk = K // nk
    est = 2 * (bm * bk * 2 + bk * bn * 2 + 2 * bm * bn * 4) + bm * bn * 4
    return pl.pallas_call(
        _mm_res_kernel,
        out_shape=jax.ShapeDtypeStruct((M, N), F32),
        grid=(M // bm, N // bn, nk),
        in_specs=[pl.BlockSpec((bm, bk), lambda i, j, k: (i, k)),
                  pl.BlockSpec((bk, bn), lambda i, j, k: (k, j)),
                  pl.BlockSpec((bm, bn), lambda i, j, k: (i, j))],
        out_specs=pl.BlockSpec((bm, bn), lambda i, j, k: (i, j)),
        scratch_shapes=[pltpu.VMEM((bm, bn), F32)],
        compiler_params=_params(("parallel", "parallel", "arbitrary"), est),
        name="matmul_res",
    )(a, b, res)


def _dsa_select_kernel(iq_ref, ikt_ref, iw_ref, mask_ref, key_ref, qpad_ref, cut_ref, *, qb, kc, seq, topk):
    i = pl.program_id(1)
    n_chunks = (i * qb + qb + kc - 1) // kc
    n_groups = kc // V7X_LANES
    idx_bits = int(math.log2(seq))

    qpad_ref[...] = jnp.zeros_like(qpad_ref)
    for h in range(IDX_HEADS):
        qpad_ref[h, :, 0:IDX_DIM] = iq_ref[:, h * IDX_DIM:(h + 1) * IDX_DIM]
    w = iw_ref[...] * (IDX_HEADS ** -0.5 * IDX_DIM ** -0.5)
    t_idx = i * qb + lax.broadcasted_iota(jnp.int32, (qb, kc), 0)
    lane_idx = lax.broadcasted_iota(jnp.int32, (qb, kc), 1)

    def score_chunk(c, carry):
        off = pl.multiple_of(c * kc, kc)
        kt = ikt_ref[:, pl.ds(off, kc)]
        acc = jnp.zeros((qb, kc), F32)
        for h in range(IDX_HEADS):
            logit = jnp.dot(qpad_ref[h], kt, preferred_element_type=F32)
            acc = acc + w[:, h:h + 1] * jnp.maximum(logit, 0.0)
        acc = jnp.where(off + lane_idx <= t_idx, acc, -jnp.inf)
        bits = pltpu.bitcast(acc, jnp.int32)
        key_ref[:, pl.ds(off, kc)] = bits ^ ((bits >> 31) & 0x7FFFFFFF)
        return carry

    lax.fori_loop(0, n_chunks, score_chunk, 0)

    def lane_fold(x):
        out = x[:, 0:V7X_LANES]
        for g in range(1, n_groups):
            out = out + x[:, g * V7X_LANES:(g + 1) * V7X_LANES]
        return out

    def count(pred):
        def body(c, cnt):
            off = pl.multiple_of(c * kc, kc)
            key = key_ref[:, pl.ds(off, kc)]
            return cnt + lane_fold(jnp.where(pred(key, off + lane_idx), 1, 0))
        cnt = lax.fori_loop(0, n_chunks, body, jnp.zeros((qb, V7X_LANES), jnp.int32))
        return jnp.sum(cnt, axis=1, keepdims=True)

    nonneg = count(lambda key, _: key >= 0)
    thr0 = jnp.where(nonneg >= topk, 0, INT32_MIN).astype(jnp.int32)

    def bit_step(it, thr):
        trial = thr | (jnp.int32(1) << (30 - it))
        return jnp.where(count(lambda key, _: key >= trial) >= topk, trial, thr)

    thr = lax.fori_loop(0, 31, bit_step, thr0)

    n_gt = count(lambda key, _: key > thr)
    n_ge = count(lambda key, _: key >= thr)
    need = topk - n_gt
    excess = jnp.logical_and(n_ge > topk, thr > KEY_OF_NEG_INF)
    cut_ref[...] = jnp.full(cut_ref.shape, seq, jnp.int32)

    @pl.when(jnp.max(excess.astype(jnp.int32)) > 0)
    def _():
        def idx_step(it, p):
            trial = p + (jnp.int32(1) << (idx_bits - 1 - it))
            f = count(lambda key, s: jnp.logical_and(key == thr, s < trial))
            return jnp.where(f < need, trial, p)
        p = lax.fori_loop(0, idx_bits, idx_step, jnp.zeros((qb, 1), jnp.int32))
        cut = jnp.where(excess, p, seq)
        cut_ref[...] = jnp.broadcast_to(cut, cut_ref.shape)

    cut = cut_ref[:, 0:1]

    def emit_chunk(c, carry):
        off = pl.multiple_of(c * kc, kc)
        key = key_ref[:, pl.ds(off, kc)]
        s_idx = off + lane_idx
        tie_ok = jnp.logical_and(key == thr, s_idx <= cut)
        sel = jnp.logical_and(jnp.logical_or(key > thr, tie_ok), s_idx <= t_idx)
        mask_ref[:, pl.ds(off, kc)] = jnp.where(sel, 0.0, MASK_NEG).astype(mask_ref.dtype)
        return carry

    lax.fori_loop(0, n_chunks, emit_chunk, 0)

    def fill_chunk(c, carry):
        off = pl.multiple_of(c * kc, kc)
        mask_ref[:, pl.ds(off, kc)] = jnp.full((qb, kc), MASK_NEG, mask_ref.dtype)
        return carry

    lax.fori_loop(n_chunks, seq // kc, fill_chunk, 0)


def _dsa_select(z_a, iq_col_block, ikt, iw, batch, seq):
    T = z_a.shape[0]
    qb = _pick(seq, (128,))
    kc = _pick(seq, (512, 256, 128))
    topk = min(TOPK_MAX, seq // 4)
    nq = seq // qb
    iq_w = IDX_HEADS * IDX_DIM
    est = 2 * (qb * iq_w * 2 + V7X_LANES * seq * 2 + qb * V7X_LANES * 4 + qb * seq * 2) \
        + qb * seq * 4 + IDX_HEADS * qb * V7X_LANES * 2 + 8 * qb * kc * 4
    kern = functools.partial(_dsa_select_kernel, qb=qb, kc=kc, seq=seq, topk=topk)
    return pl.pallas_call(
        kern,
        out_shape=jax.ShapeDtypeStruct((T, seq), BF16),
        grid=(batch, nq),
        in_specs=[pl.BlockSpec((qb, iq_w), lambda b, i: (b * nq + i, iq_col_block)),
                  pl.BlockSpec((None, V7X_LANES, seq), lambda b, i: (b, 0, 0)),
                  pl.BlockSpec((qb, IDX_HEADS), lambda b, i: (b * nq + i, 0))],
        out_specs=pl.BlockSpec((qb, seq), lambda b, i: (b * nq + i, 0)),
        scratch_shapes=[pltpu.VMEM((qb, seq), jnp.int32),
                        pltpu.VMEM((IDX_HEADS, qb, V7X_LANES), BF16),
                        pltpu.VMEM((qb, V7X_LANES), jnp.int32)],
        compiler_params=_params(("parallel", "parallel"), est),
        name="dsa_select",
    )(z_a, ikt, iw)


def _dsa_attn_kernel(q_ref, k_ref, v_ref, mask_ref, bias_ref, o_ref, acc_ref, m_ref, l_ref, *, heads, blk, scale):
    i = pl.program_id(1)
    j = pl.program_id(2)
    hd = A_HEAD_DIM

    @pl.when(j == 0)
    def _():
        acc_ref[...] = jnp.zeros_like(acc_ref)
        l_ref[...] = jnp.zeros_like(l_ref)
        m_ref[...] = jnp.full(m_ref.shape, MASK_NEG, F32)

    def step(with_bias):
        mask = mask_ref[...].astype(F32)
        for h in range(heads):
            hs = slice(h * hd, (h + 1) * hd)
            s = lax.dot_general(q_ref[:, hs], k_ref[:, hs], (((1,), (1,)), ((), ())),
                                preferred_element_type=F32) * scale + mask
            if with_bias:
                row = bias_ref[jnp.where(j == i, 0, 1), h]
                tile = pltpu.roll(jnp.broadcast_to(row, (blk, 2 * blk)), 0, 1, stride=1, stride_axis=0)
                s = s + tile[:, 0:blk]
            m_prev = m_ref[h]
            l_prev = l_ref[h]
            m_new = jnp.maximum(m_prev, jnp.max(s, axis=1, keepdims=True))
            alpha = jnp.exp(m_prev - m_new)
            p = jnp.exp(s - m_new[:, 0:1])
            l_ref[h] = alpha * l_prev + jnp.sum(p, axis=1, keepdims=True)
            m_ref[h] = m_new
            acc_ref[:, hs] = alpha * acc_ref[:, hs] + jnp.dot(p.astype(BF16), v_ref[:, hs],
                                                             preferred_element_type=F32)

    @pl.when(j < i - 1)
    def _():
        step(False)

    @pl.when(jnp.logical_and(j >= i - 1, j <= i))
    def _():
        step(True)

    @pl.when(j == pl.num_programs(2) - 1)
    def _():
        for h in range(heads):
            hs = slice(h * hd, (h + 1) * hd)
            o_ref[:, hs] = (acc_ref[:, hs] / l_ref[h]).astype(o_ref.dtype)


def _dsa_attention(z_a, mask, bias_tab, batch, seq, a_width):
    T = z_a.shape[0]
    heads = a_width // A_HEAD_DIM
    blk = _pick(seq, (512, 256, 128))
    nb = seq // blk
    scale = A_HEAD_DIM ** -0.5
    est = 2 * (3 * blk * a_width * 2 + blk * blk * 2 + blk * a_width * 2) + bias_tab.size * 4 * 2 \
        + blk * a_width * 4 + 2 * heads * blk * V7X_LANES * 4 + heads * blk * blk * 4 + blk * 2 * blk * 4
    kern = functools.partial(_dsa_attn_kernel, heads=heads, blk=blk, scale=scale)

    def kv_map(col):
        return lambda b, i, j: (b * nb + jnp.minimum(j, i), col)

    return pl.pallas_call(
        kern,
        out_shape=jax.ShapeDtypeStruct((T, a_width), BF16),
        grid=(batch, nb, nb),
        in_specs=[pl.BlockSpec((blk, a_width), lambda b, i, j: (b * nb + i, 0)),
                  pl.BlockSpec((blk, a_width), kv_map(1)),
                  pl.BlockSpec((blk, a_width), kv_map(2)),
                  pl.BlockSpec((blk, blk), lambda b, i, j: (b * nb + i, jnp.minimum(j, i))),
                  pl.BlockSpec(bias_tab.shape, lambda b, i, j: (0, 0, 0, 0))],
        out_specs=pl.BlockSpec((blk, a_width), lambda b, i, j: (b * nb + i, 0)),
        scratch_shapes=[pltpu.VMEM((blk, a_width), F32),
                        pltpu.VMEM((heads, blk, V7X_LANES), F32),
                        pltpu.VMEM((heads, blk, V7X_LANES), F32)],
        compiler_params=_params(("parallel", "parallel", "arbitrary"), est),
        name="dsa_attention",
    )(z_a, z_a, z_a, mask, bias_tab)


def _t5_bucket_np(rel):
    max_exact = REL_BUCKETS // 2
    n = np.maximum(rel, 0)
    nf = np.maximum(n, 1).astype(np.float32)
    large = max_exact + (np.log(nf / np.float32(max_exact)) / np.float32(math.log(REL_MAX_DIST / max_exact))
                         * np.float32(REL_BUCKETS - max_exact)).astype(np.int32)
    large = np.minimum(large, REL_BUCKETS - 1)
    return np.where(n < max_exact, n, large)


def _bias_tables(rel_bias, blk):
    w = 2 * blk
    y = np.arange(w)
    diff = np.where(y < blk, y, y - w)
    tabs = []
    for off in (0, blk):
        rel = off - diff
        bucket = _t5_bucket_np(rel)
        near = np.logical_and(rel >= 0, rel < REL_MAX_DIST)
        idx = np.where(near, bucket, REL_BUCKETS - 1)
        tabs.append(idx)
    idx = jnp.asarray(np.stack(tabs))
    far = rel_bias[REL_BUCKETS - 1]
    tab = rel_bias[idx] - far
    return jnp.transpose(tab, (0, 2, 1))[:, :, None, :].astype(F32)


def _pool_kernel(u_ref, w_ref, sc_ref, o_ref, ext_ref, *, ts, groups, gdim, halo):
    s = pl.program_id(1)

    @pl.when(s == 0)
    def _():
        ext_ref[0:halo, :] = jnp.zeros((halo, ext_ref.shape[1]), F32)

    ext_ref[halo:halo + ts, :] = u_ref[...]
    t = s * ts + lax.broadcasted_iota(jnp.int32, (ts, 1), 0)
    for g in range(groups):
        win = POOL_WINDOWS[g]
        cs = slice(g * gdim, (g + 1) * gdim)
        u = ext_ref[halo:halo + ts, cs]
        tot = u
        for d in range(1, win):
            tot = tot + ext_ref[halo - d:halo - d + ts, cs]
        cnt = jnp.minimum(t + 1, win).astype(F32)
        pooled = tot / cnt - u
        mixed = jnp.dot(pooled.astype(BF16), w_ref[g], preferred_element_type=F32)
        o_ref[:, cs] = (mixed * sc_ref[:, cs]).astype(o_ref.dtype)
    ext_ref[0:halo, :] = ext_ref[ts:ts + halo, :]


def _pool(z_b, pool_w, pool_scale, batch, seq, width):
    T = z_b.shape[0]
    groups = len(POOL_WINDOWS)
    gdim = width // groups
    halo = 16
    ts = _pick(seq, (1024, 512, 256, 128))
    ns = seq // ts
    est = 2 * (ts * width * 4 + ts * width * 2 + groups * gdim * gdim * 2) + (ts + halo) * width * 4 + 6 * ts * gdim * 4
    kern = functools.partial(_pool_kernel, ts=ts, groups=groups, gdim=gdim, halo=halo)
    return pl.pallas_call(
        kern,
        out_shape=jax.ShapeDtypeStruct((T, width), BF16),
        grid=(batch, ns),
        in_specs=[pl.BlockSpec((ts, width), lambda b, s: (b * ns + s, 0)),
                  pl.BlockSpec((groups, gdim, gdim), lambda b, s: (0, 0, 0)),
                  pl.BlockSpec((1, width), lambda b, s: (0, 0))],
        out_specs=pl.BlockSpec((ts, width), lambda b, s: (b * ns + s, 0)),
        scratch_shapes=[pltpu.VMEM((ts + halo, width), F32)],
        compiler_params=_params(("arbitrary", "arbitrary"), est),
        name="pool",
    )(z_b, pool_w.astype(BF16), pool_scale.reshape(1, width).astype(F32))


def _retention_kernel(q_ref, k_ref, v_ref, g_ref, cos_ref, sin_ref, dm_ref, qd_ref, kd_ref, cd_ref,
                      o_ref, state_ref, *, batch, heads, hdim):
    c = pl.program_id(0)
    half = hdim // 2

    @pl.when(c == 0)
    def _():
        state_ref[...] = jnp.zeros_like(state_ref)

    cos = cos_ref[...]
    sin = sin_ref[...]

    def rot(x):
        x1 = x[:, 0:half]
        x2 = x[:, half:hdim]
        return jnp.concatenate([x1 * cos - x2 * sin, x1 * sin + x2 * cos], axis=1)

    for b in range(batch):
        for h in range(heads):
            hs = slice(h * hdim, (h + 1) * hdim)
            qc = rot(q_ref[b, :, hs])
            kc = rot(k_ref[b, :, hs]) * (hdim ** -0.5)
            vc = v_ref[b, :, hs].astype(BF16)
            state = state_ref[b, h]
            s = lax.dot_general(qc.astype(BF16), kc.astype(BF16), (((1,), (1,)), ((), ())),
                                preferred_element_type=F32) * dm_ref[h]
            inner = jnp.dot(s.astype(BF16), vc, preferred_element_type=F32)
            cross = jnp.dot((qc * qd_ref[h]).astype(BF16), state.astype(BF16), preferred_element_type=F32)
            kdt = jnp.transpose(kc * kd_ref[h]).astype(BF16)
            state_ref[b, h] = state * cd_ref[h] + jnp.dot(kdt, vc, preferred_element_type=F32)
            ret = inner + cross
            ret = ret * lax.rsqrt(jnp.mean(ret * ret, axis=-1, keepdims=True) + NORM_EPS)
            gate = g_ref[b, :, hs]
            o_ref[b, :, hs] = (ret * (gate * jax.nn.sigmoid(gate))).astype(o_ref.dtype)


def _retention(z_b, batch, seq, width):
    heads = RET_HEADS
    hdim = width // heads
    half = hdim // 2
    C = RET_CHUNK
    nc = seq // C
    zb3 = z_b.reshape(batch, seq, z_b.shape[1])

    inv_freq = 1.0 / (ROPE_BASE ** jnp.linspace(0.0, 1.0, half, dtype=F32))
    ang = jnp.arange(seq, dtype=F32)[:, None] * inv_freq[None, :]
    cos, sin = jnp.cos(ang), jnp.sin(ang)
    log_g = jnp.log(1.0 - jnp.exp2(-5.0 - jnp.arange(heads, dtype=F32)))
    ii = jnp.arange(C, dtype=F32)
    diff = ii[:, None] - ii[None, :]
    dmask = jnp.where(diff[None] >= 0, jnp.exp(jnp.maximum(diff, 0.0)[None] * log_g[:, None, None]), 0.0)
    q_decay = jnp.exp((ii + 1.0)[None, :, None] * log_g[:, None, None])
    k_decay = jnp.exp((C - 1.0 - ii)[None, :, None] * log_g[:, None, None])
    qd = jnp.broadcast_to(q_decay, (heads, C, hdim)).astype(F32)
    kd = jnp.broadcast_to(k_decay, (heads, C, hdim)).astype(F32)
    cd = jnp.broadcast_to(jnp.exp(C * log_g)[:, None, None], (heads, 1, hdim)).astype(F32)

    est = 2 * (4 * batch * C * width * 4 + 2 * C * half * 4 + heads * C * C * 4 + 2 * heads * C * hdim * 4
               + batch * C * width * 2) + batch * heads * hdim * hdim * 4 + 16 * C * hdim * 4
    kern = functools.partial(_retention_kernel, batch=batch, heads=heads, hdim=hdim)

    def col(cb):
        return pl.BlockSpec((batch, C, width), lambda c: (0, c, cb))

    def whole(a):
        return pl.BlockSpec(a.shape, lambda c: (0,) * a.ndim)

    out = pl.pallas_call(
        kern,
        out_shape=jax.ShapeDtypeStruct((batch, seq, width), BF16),
        grid=(nc,),
        in_specs=[col(1), col(2), col(3), col(4),
                  pl.BlockSpec((C, half), lambda c: (c, 0)), pl.BlockSpec((C, half), lambda c: (c, 0)),
                  whole(dmask), whole(qd), whole(kd), whole(cd)],
        out_specs=pl.BlockSpec((batch, C, width), lambda c: (0, c, 0)),
        scratch_shapes=[pltpu.VMEM((batch, heads, hdim, hdim), F32)],
        compiler_params=_params(("arbitrary",), est),
        name="retention",
    )(zb3, zb3, zb3, zb3, cos, sin, dmask, qd, kd, cd)
    return out.reshape(batch * seq, width)


def _merge_kernel(oa_ref, ob_ref, oc_ref, wa_ref, wb_ref, wc_ref, ga_ref, gb_ref, gc_ref, o_ref):
    ya = jnp.dot(oa_ref[...], wa_ref[...], preferred_element_type=F32)
    m = jax.nn.sigmoid(ga_ref[...]) * ya
    yb = jnp.dot(ob_ref[...], wb_ref[...], preferred_element_type=F32)
    m = m + jax.nn.sigmoid(gb_ref[...]) * yb
    yc = jnp.dot(oc_ref[...], wc_ref[...], preferred_element_type=F32)
    m = m + jax.nn.sigmoid(gc_ref[...]) * yc
    o_ref[...] = m.astype(o_ref.dtype)


def _merge(o_a, o_b, o_c, w_a, w_b, w_c, z_g, d_model):
    T = o_a.shape[0]
    ka, kb, kc = o_a.shape[1], o_b.shape[1], o_c.shape[1]
    bm = _pick(T, (512, 256, 128))
    bn = _pick(d_model, (512, 256, 128))
    nn = d_model // bn
    est = 2 * (bm * (ka + kb + kc) * 2 + (ka + kb + kc) * bn * 2 + 3 * bm * bn * 4 + bm * bn * 2) + 4 * bm * bn * 4

    def gate(br):
        return pl.BlockSpec((bm, bn), lambda i, j: (i, br * nn + j))

    return pl.pallas_call(
        _merge_kernel,
        out_shape=jax.ShapeDtypeStruct((T, d_model), BF16),
        grid=(T // bm, nn),
        in_specs=[pl.BlockSpec((bm, ka), lambda i, j: (i, 0)),
                  pl.BlockSpec((bm, kb), lambda i, j: (i, 0)),
                  pl.BlockSpec((bm, kc), lambda i, j: (i, 0)),
                  pl.BlockSpec((ka, bn), lambda i, j: (0, j)),
                  pl.BlockSpec((kb, bn), lambda i, j: (0, j)),
                  pl.BlockSpec((kc, bn), lambda i, j: (0, j)),
                  gate(0), gate(1), gate(2)],
        out_specs=pl.BlockSpec((bm, bn), lambda i, j: (i, j)),
        compiler_params=_params(("parallel", "parallel"), est),
        name="merge",
    )(o_a, o_b, o_c, w_a, w_b, w_c, z_g, z_g, z_g)


def _ffn_up_kernel(x_ref, wa_ref, wb_ref, cw_ref, o_ref, ext_ref, *, bm, tiles_per_seq):
    m = pl.program_id(1)
    pad = V7X_SUBLANES

    @pl.when(m % tiles_per_seq == 0)
    def _():
        ext_ref[0:pad, :] = jnp.zeros((pad, ext_ref.shape[1]), F32)

    x = x_ref[...]
    a = jnp.dot(x, wa_ref[...], preferred_element_type=F32)
    ext_ref[pad:pad + bm, :] = a
    a1 = ext_ref[pad - 1:pad - 1 + bm, :]
    a2 = ext_ref[pad - 2:pad - 2 + bm, :]
    conv = cw_ref[0:1, :] * a2 + cw_ref[1:2, :] * a1 + cw_ref[2:3, :] * a
    b = jnp.dot(x, wb_ref[...], preferred_element_type=F32)
    o_ref[...] = (conv * jax.nn.sigmoid(conv) * b).astype(o_ref.dtype)
    ext_ref[0:pad, :] = ext_ref[bm:bm + pad, :]


def _ffn_up(h, w_a, w_b, conv_w, seq):
    T, K = h.shape
    N = w_a.shape[1]
    bm = _pick(seq, (512, 256, 128))
    bn = _pick(N, (512, 256, 128))
    est = 2 * (bm * K * 2 + 2 * K * bn * 2 + bm * bn * 2 + 8 * bn * 4) + (bm + 8) * bn * 4 + 6 * bm * bn * 4
    kern = functools.partial(_ffn_up_kernel, bm=bm, tiles_per_seq=seq // bm)
    return pl.pallas_call(
        kern,
        out_shape=jax.ShapeDtypeStruct((T, N), BF16),
        grid=(N // bn, T // bm),
        in_specs=[pl.BlockSpec((bm, K), lambda j, m: (m, 0)),
                  pl.BlockSpec((K, bn), lambda j, m: (0, j)),
                  pl.BlockSpec((K, bn), lambda j, m: (0, j)),
                  pl.BlockSpec((conv_w.shape[0], bn), lambda j, m: (0, j))],
        out_specs=pl.BlockSpec((bm, bn), lambda j, m: (m, j)),
        scratch_shapes=[pltpu.VMEM((bm + V7X_SUBLANES, bn), F32)],
        compiler_params=_params(("arbitrary", "arbitrary"), est),
        name="ffn_up",
    )(h, w_a, w_b, conv_w)


def _pad_cols(w, n):
    return w if w.shape[-1] == n else jnp.pad(w, [(0, 0)] * (w.ndim - 1) + [(0, n - w.shape[-1])])


def kernel(x, norm1_g, w_in, w_br_a, pool_w, pool_scale, w_br_b, w_br_c, w_out, rel_bias, norm2_g,
           w_ffn_up, ffn_conv, w_ffn_down, final_g):
    B, S, D = x.shape
    depth = w_in.shape[0]
    T = B * S
    aw = w_br_a.shape[1]
    pw = w_br_b.shape[1]
    rw = w_br_c.shape[1]
    assert pw == rw, "pool and retention slices are addressed as equal-width column blocks"
    iq_w = IDX_HEADS * IDX_DIM
    assert aw % iq_w == 0
    dff = w_ffn_down.shape[1]
    dff_p = -(-dff // 1024) * 1024
    sizes = (aw, aw, aw, iq_w, IDX_DIM, IDX_HEADS, pw, rw, rw, rw, rw, N_BRANCHES * D)
    offs = np.concatenate([[0], np.cumsum(sizes)]).tolist()
    assert offs[-1] == w_in.shape[2]

    attn_blk = _pick(S, (512, 256, 128))
    bias_tab = _bias_tables(rel_bias.astype(F32), attn_blk)

    xf = x.reshape(T, D).astype(F32)
    for l in range(depth):
        wl = w_in[l]
        w_a = wl[:, offs[0]:offs[4]].astype(BF16)
        w_i = _pad_cols(wl[:, offs[4]:offs[6]], V7X_LANES).astype(BF16)
        w_b = wl[:, offs[6]:offs[11]].astype(BF16)
        w_g = wl[:, offs[11]:offs[12]].astype(BF16)

        h = _rmsnorm(xf, norm1_g[l], BF16)
        z_a = _matmul(h, w_a, BF16)
        z_i = _matmul(h, w_i, F32)
        z_b = _matmul(h, w_b, F32)
        z_g = _matmul(h, w_g, F32)

        ik = z_i[:, 0:IDX_DIM].reshape(B, S, IDX_DIM)
        ikt = jnp.pad(jnp.transpose(ik, (0, 2, 1)), ((0, 0), (0, V7X_LANES - IDX_DIM), (0, 0))).astype(BF16)
        iw = z_i[:, IDX_DIM:IDX_DIM + IDX_HEADS]
        mask = _dsa_select(z_a, 3 * aw // iq_w, ikt, iw, B, S)
        o_a = _dsa_attention(z_a, mask, bias_tab, B, S, aw)
        o_b = _pool(z_b, pool_w[l], pool_scale[l], B, S, pw)
        o_c = _retention(z_b, B, S, rw)

        m = _merge(o_a, o_b, o_c, w_br_a[l].astype(BF16), w_br_b[l].astype(BF16), w_br_c[l].astype(BF16), z_g, D)
        xf = _matmul_res(m, w_out[l].astype(BF16), xf)

        h2 = _rmsnorm(xf, norm2_g[l], BF16)
        w_up = w_ffn_up[l]
        w_ua = _pad_cols(w_up[:, 0:dff], dff_p).astype(BF16)
        w_ub = _pad_cols(w_up[:, dff:2 * dff], dff_p).astype(BF16)
        cw = _pad_cols(ffn_conv[l], dff_p).astype(F32)
        w_dn = jnp.pad(w_ffn_down[l], ((0, dff_p - dff), (0, 0))).astype(BF16)
        g = _ffn_up(h2, w_ua, w_ub, cw, S)
        xf = _matmul_res(g, w_dn, xf)

    out = _rmsnorm(xf, final_g, x.dtype)
    return out.reshape(B, S, D)
```
